```python
import jax, jax.numpy as jnp
from jax import lax
import numpy as np

D_MODEL = 4096
BATCH = 1
SEQ = 16384
DEPTH = 4

CHUNK = 64
N_MIXERS = 2
N_MEM = 256
MIX_WIDTH = D_MODEL
MEM_HEADS = 4
MEM_WIDTH = MIX_WIDTH // 4
MEM_HEAD_DIM = MEM_WIDTH // MEM_HEADS
BRANCH_WIDTH = MIX_WIDTH - MEM_WIDTH
CONV_WIDTH = 3
GMLP_BLOCK = 128
GMLP_GROUPS = 8
GMLP_GROUP_DIM = BRANCH_WIDTH // GMLP_GROUPS
CONV_IN_WIDTH = 3 * BRANCH_WIDTH + MEM_WIDTH + MIX_WIDTH
GMLP_IN_WIDTH = 2 * BRANCH_WIDTH + MEM_WIDTH + MIX_WIDTH
N_CONV_LAYERS = (DEPTH + 1) // 2
N_GMLP_LAYERS = DEPTH // 2
RMS_EPS = 1e-6
LN_EPS = 1e-5

kernel_name = "hybrid_conv_gmlp_memory_trunk"


def rms_norm(x, g):
    xf = x.astype(jnp.float32)
    y = xf * lax.rsqrt(jnp.mean(xf * xf, axis=-1, keepdims=True) + RMS_EPS)
    return (y * g.astype(jnp.float32)).astype(x.dtype)


def layer_norm(x, g, b):
    xf = x.astype(jnp.float32)
    mu = jnp.mean(xf, axis=-1, keepdims=True)
    xc = xf - mu
    y = xc * lax.rsqrt(jnp.mean(xc * xc, axis=-1, keepdims=True) + LN_EPS)
    return (y * g.astype(jnp.float32) + b.astype(jnp.float32)).astype(x.dtype)


def memory_attention(q, mem, mem_g, w_kv):
    b, s, _ = q.shape
    m = mem.shape[1]
    kv = rms_norm(mem, mem_g) @ w_kv
    k, v = jnp.split(kv, 2, axis=-1)
    q = q.reshape(b, s, MEM_HEADS, MEM_HEAD_DIM)
    k = k.reshape(b, m, MEM_HEADS, MEM_HEAD_DIM)
    v = v.reshape(b, m, MEM_HEADS, MEM_HEAD_DIM)
    scores = jnp.einsum('bshd,bmhd->bhsm', q, k).astype(jnp.float32) * (MEM_HEAD_DIM ** -0.5)
    p = jax.nn.softmax(scores, axis=-1).astype(v.dtype)
    o = jnp.einsum('bhsm,bmhd->bshd', p, v)
    return o.reshape(b, s, MEM_WIDTH)


def causal_short_conv(h, w):
    s = h.shape[1]
    hp = jnp.pad(h, ((0, 0), (CONV_WIDTH - 1, 0), (0, 0)))
    out = hp[:, 0:s] * w[0]
    for k in range(1, CONV_WIDTH):
        out = out + hp[:, k:k + s] * w[k]
    return out


def conv_branch(paths, w_conv):
    b_gate, c_gate, h = jnp.split(paths, 3, axis=-1)
    return b_gate * causal_short_conv(c_gate * h, w_conv)


def gmlp_branch(paths, ln_g, ln_b, w_s, b_s):
    z = jax.nn.gelu(paths)
    u, v = jnp.split(z, 2, axis=-1)
    v = layer_norm(v, ln_g, ln_b)
    b, s, _ = v.shape
    n_blocks = s // GMLP_BLOCK
    v = v.reshape(b, n_blocks, GMLP_BLOCK, GMLP_GROUPS, GMLP_GROUP_DIM)
    mask = jnp.tril(jnp.ones((GMLP_BLOCK, GMLP_BLOCK), dtype=bool))
    w = jnp.where(mask[None], w_s, jnp.zeros_like(w_s))
    f = jnp.einsum('gts,bnsgd->bntgd', w, v) + jnp.transpose(b_s)[None, None, :, :, None]
    return u * f.reshape(b, s, BRANCH_WIDTH)


def setup_inputs(seed: int = 0) -> dict:
    key = jax.random.key(seed)
    ks = jax.random.split(key, 16)
    f32 = jnp.float32
    x = jax.random.normal(ks[0], (BATCH, SEQ, D_MODEL), f32)
    mem = jax.random.normal(ks[1], (BATCH, N_MEM, D_MODEL), f32)
    pre_norm_g = 1.0 + 0.1 * jax.random.normal(ks[2], (DEPTH, D_MODEL), f32)
    post_norm_g = 1.0 + 0.1 * jax.random.normal(ks[3], (DEPTH, D_MODEL), f32)
    mem_norm_g = 1.0 + 0.1 * jax.random.normal(ks[4], (DEPTH, D_MODEL), f32)
    w_mem_kv = jax.random.normal(ks[5], (DEPTH, D_MODEL, 2 * MEM_WIDTH), f32) * D_MODEL ** -0.5
    w_out = jax.random.normal(ks[6], (DEPTH, MIX_WIDTH, D_MODEL), f32) * MIX_WIDTH ** -0.5
    conv_w_in = jax.random.normal(ks[7], (N_CONV_LAYERS, D_MODEL, CONV_IN_WIDTH), f32) * D_MODEL ** -0.5
    conv_w = jax.random.normal(ks[8], (N_CONV_LAYERS, CONV_WIDTH, BRANCH_WIDTH), f32) * CONV_WIDTH ** -0.5
    gmlp_w_in = jax.random.normal(ks[9], (N_GMLP_LAYERS, D_MODEL, GMLP_IN_WIDTH), f32) * D_MODEL ** -0.5
    gmlp_ln_g = 1.0 + 0.1 * jax.random.normal(ks[10], (N_GMLP_LAYERS, BRANCH_WIDTH), f32)
    gmlp_ln_b = 0.02 * jax.random.normal(ks[11], (N_GMLP_LAYERS, BRANCH_WIDTH), f32)
    gmlp_w_s = jax.random.normal(ks[12], (N_GMLP_LAYERS, GMLP_GROUPS, GMLP_BLOCK, GMLP_BLOCK), f32) * GMLP_BLOCK ** -0.5
    gmlp_b_s = 1.0 + 0.1 * jax.random.normal(ks[13], (N_GMLP_LAYERS, GMLP_GROUPS, GMLP_BLOCK), f32)
    return {"x": x, "mem": mem, "pre_norm_g": pre_norm_g, "post_norm_g": post_norm_g,
            "mem_norm_g": mem_norm_g, "w_mem_kv": w_mem_kv, "w_out": w_out,
            "conv_w_in": conv_w_in, "conv_w": conv_w, "gmlp_w_in": gmlp_w_in,
            "gmlp_ln_g": gmlp_ln_g, "gmlp_ln_b": gmlp_ln_b, "gmlp_w_s": gmlp_w_s,
            "gmlp_b_s": gmlp_b_s}


def reference(x, mem, pre_norm_g, post_norm_g, mem_norm_g, w_mem_kv, w_out,
              conv_w_in, conv_w, gmlp_w_in, gmlp_ln_g, gmlp_ln_b, gmlp_w_s, gmlp_b_s):
    for i in range(DEPTH):
        h = rms_norm(x, pre_norm_g[i])
        j = i // N_MIXERS
        if i % N_MIXERS == 0:
            proj = h @ conv_w_in[j]
            paths, q, z = jnp.split(proj, [3 * BRANCH_WIDTH, 3 * BRANCH_WIDTH + MEM_WIDTH], axis=-1)
            branch = conv_branch(paths, conv_w[j])
        else:
            proj = h @ gmlp_w_in[j]
            paths, q, z = jnp.split(proj, [2 * BRANCH_WIDTH, 2 * BRANCH_WIDTH + MEM_WIDTH], axis=-1)
            branch = gmlp_branch(paths, gmlp_ln_g[j], gmlp_ln_b[j], gmlp_w_s[j], gmlp_b_s[j])
        mem_out = memory_attention(q, mem, mem_norm_g[i], w_mem_kv[i])
        y = jnp.concatenate([branch, mem_out], axis=-1) * jax.nn.silu(z)
        y = y @ w_out[i]
        x = x + rms_norm(y, post_norm_g[i])
    return x
```

```python
import functools

import jax
import jax.numpy as jnp
from jax import lax
from jax.experimental import pallas as pl
from jax.experimental.pallas import tpu as pltpu

F32 = jnp.float32
BF16 = jnp.bfloat16

RMS_EPS = 1e-6
LN_EPS = 1e-5

LANE = 128
SUBLANE = 8
COL_BLOCK = 256
VMEM_LIMIT_BYTES = 56 * 2**20

MEM_HEADS = 4
CONV_WIDTH = 3
GMLP_BLOCK = 128
GMLP_GROUPS = 8


def _dot(a, b):
    return jnp.dot(a, b, preferred_element_type=F32)


def _silu(z):
    return z * (1.0 / (1.0 + jnp.exp(-z)))


def _compiler_params(n_axes):
    return pltpu.CompilerParams(
        dimension_semantics=("arbitrary",) * n_axes,
        vmem_limit_bytes=VMEM_LIMIT_BYTES,
    )


def _prenorm_kernel(x_ref, g_ref, xg_ref, rstd_ref):
    xf = x_ref[...]
    rstd_ref[...] = lax.rsqrt(jnp.mean(xf * xf, axis=-1, keepdims=True) + RMS_EPS)
    xg_ref[...] = (xf * g_ref[...]).astype(BF16)


def _prenorm(x, g, rows=256):
    s, d = x.shape
    return pl.pallas_call(
        _prenorm_kernel,
        grid=(s // rows,),
        in_specs=[
            pl.BlockSpec((rows, d), lambda i: (i, 0)),
            pl.BlockSpec((1, d), lambda i: (0, 0)),
        ],
        out_specs=[
            pl.BlockSpec((rows, d), lambda i: (i, 0)),
            pl.BlockSpec((rows, 1), lambda i: (i, 0)),
        ],
        out_shape=[
            jax.ShapeDtypeStruct((s, d), BF16),
            jax.ShapeDtypeStruct((s, 1), F32),
        ],
        compiler_params=_compiler_params(1),
        name="prenorm",
    )(x, g.reshape(1, d))


def _kv_kernel(mem_ref, g_ref, w_ref, kv_ref):
    mf = mem_ref[...]
    y = mf * lax.rsqrt(jnp.mean(mf * mf, axis=-1, keepdims=True) + RMS_EPS)
    y = (y * g_ref[...]).astype(BF16)
    kv_ref[...] = _dot(y, w_ref[...].astype(BF16)).astype(BF16)


def _memory_kv(mem, mem_norm_g, w_mem_kv):
    m, d = mem.shape
    depth, _, kv_width = w_mem_kv.shape
    kv = pl.pallas_call(
        _kv_kernel,
        grid=(depth, kv_width // COL_BLOCK),
        in_specs=[
            pl.BlockSpec((m, d), lambda l, n: (0, 0)),
            pl.BlockSpec((None, 1, d), lambda l, n: (l, 0, 0)),
            pl.BlockSpec((None, d, COL_BLOCK), lambda l, n: (l, 0, n)),
        ],
        out_specs=pl.BlockSpec((None, m, COL_BLOCK), lambda l, n: (l, 0, n)),
        out_shape=jax.ShapeDtypeStruct((depth, m, kv_width), BF16),
        compiler_params=_compiler_params(2),
        name="memory_kv",
    )(mem, mem_norm_g.reshape(depth, 1, d), w_mem_kv)
    head_dim = kv_width // (2 * MEM_HEADS)
    return kv.reshape(depth, m, 2 * MEM_HEADS, head_dim).transpose(0, 2, 1, 3)


def _memory_read(q, z, kvh_ref, head):
    k = kvh_ref[head]
    v = kvh_ref[MEM_HEADS + head]
    scale = k.shape[-1] ** -0.5
    s = lax.dot_general(q.astype(BF16), k, (((1,), (1,)), ((), ())), preferred_element_type=F32) * scale
    e = jnp.exp(s - jnp.max(s, axis=-1, keepdims=True))
    o = _dot(e.astype(BF16), v) / jnp.sum(e, axis=-1, keepdims=True)
    return o * _silu(z)


def _conv_mixer_kernel(n_branch, xg_ref, rstd_ref, wa_ref, wb_ref, wc_ref, wz_ref, cw_ref, kvh_ref,
                       y_ref, ubuf_ref, carry_ref):
    i = pl.program_id(0)
    j = pl.program_id(1)
    rows = xg_ref.shape[0]

    @pl.when(jnp.logical_and(i == 0, j < n_branch))
    def _():
        carry_ref[j] = jnp.zeros(carry_ref.shape[1:], F32)

    @pl.when(j < n_branch)
    def _():
        h = xg_ref[...]
        rs = rstd_ref[...]
        b_gate = _dot(h, wa_ref[...]) * rs
        c_gate = _dot(h, wb_ref[...]) * rs
        hid = _dot(h, wc_ref[...]) * rs
        z = _dot(h, wz_ref[...]) * rs
        u = c_gate * hid
        ubuf_ref[0:SUBLANE, :] = carry_ref[j]
        ubuf_ref[SUBLANE:SUBLANE + rows, :] = u
        carry_ref[j] = u[rows - SUBLANE:rows, :]
        u1 = ubuf_ref[SUBLANE - 1:SUBLANE - 1 + rows, :]
        u2 = ubuf_ref[SUBLANE - 2:SUBLANE - 2 + rows, :]
        w = cw_ref[j]
        conv = u2 * w[0:1, :] + u1 * w[1:2, :] + u * w[2:3, :]
        y_ref[...] = (b_gate * conv * _silu(z)).astype(BF16)

    @pl.when(j >= n_branch)
    def _():
        h = xg_ref[...]
        rs = rstd_ref[...]
        q = _dot(h, wa_ref[...]) * rs
        z = _dot(h, wz_ref[...]) * rs
        y_ref[...] = _memory_read(q, z, kvh_ref, j - n_branch).astype(BF16)


def _conv_mixer(xg, rstd, w_in, conv_w, kvh, rows=1024):
    s, d = xg.shape
    n_mix = d // COL_BLOCK
    n_branch = conv_w.shape[-1] // COL_BLOCK
    last = n_branch - 1
    z0 = 3 * n_branch + (n_mix - n_branch)
    cw = conv_w.reshape(CONV_WIDTH, n_branch, COL_BLOCK).transpose(1, 0, 2)
    wspec = lambda fn: pl.BlockSpec((d, COL_BLOCK), fn)
    return pl.pallas_call(
        functools.partial(_conv_mixer_kernel, n_branch),
        grid=(s // rows, n_mix),
        in_specs=[
            pl.BlockSpec((rows, d), lambda i, j: (i, 0)),
            pl.BlockSpec((rows, 1), lambda i, j: (i, 0)),
            wspec(lambda i, j: (0, jnp.where(j < n_branch, j, j + 2 * n_branch))),
            wspec(lambda i, j: (0, n_branch + jnp.minimum(j, last))),
            wspec(lambda i, j: (0, 2 * n_branch + jnp.minimum(j, last))),
            wspec(lambda i, j: (0, z0 + j)),
            pl.BlockSpec(cw.shape, lambda i, j: (0, 0, 0)),
            pl.BlockSpec(kvh.shape, lambda i, j: (0, 0, 0)),
        ],
        out_specs=pl.BlockSpec((rows, COL_BLOCK), lambda i, j: (i, j)),
        out_shape=jax.ShapeDtypeStruct((s, d), BF16),
        scratch_shapes=[
            pltpu.VMEM((rows + SUBLANE, COL_BLOCK), F32),
            pltpu.VMEM((n_branch, SUBLANE, COL_BLOCK), F32),
        ],
        compiler_params=_compiler_params(2),
        name="conv_mixer",
    )(xg, rstd, w_in, w_in, w_in, w_in, cw, kvh)


def _gmlp_mixer_kernel(n_branch, xg_ref, rstd_ref, wa_ref, wz_ref, lng_ref, lnb_ref, ws_ref, bs_ref,
                       kvh_ref, y_ref, vbuf_ref, fbuf_ref, s1_ref, s2_ref):
    j = pl.program_id(1)
    rows = xg_ref.shape[0]
    width = n_branch * COL_BLOCK

    @pl.when(j < n_branch)
    def _():
        v = jax.nn.gelu(_dot(xg_ref[...], wa_ref[...]) * rstd_ref[...])
        vbuf_ref[j] = v
        p1 = jnp.sum(v, axis=-1, keepdims=True)
        p2 = jnp.sum(v * v, axis=-1, keepdims=True)

        @pl.when(j == 0)
        def _():
            s1_ref[...] = p1
            s2_ref[...] = p2

        @pl.when(j > 0)
        def _():
            s1_ref[...] += p1
            s2_ref[...] += p2

    @pl.when(jnp.logical_and(j >= n_branch, j < 2 * n_branch))
    def _():
        jj = j - n_branch
        h = xg_ref[...]
        rs = rstd_ref[...]
        u = jax.nn.gelu(_dot(h, wa_ref[...]) * rs)
        z = _dot(h, wz_ref[...]) * rs
        mu = s1_ref[...] * (1.0 / width)
        var = s2_ref[...] * (1.0 / width) - mu * mu
        vn = (vbuf_ref[jj] - mu) * lax.rsqrt(var + LN_EPS) * lng_ref[jj] + lnb_ref[jj]
        vn = vn.astype(BF16)
        tril = (lax.broadcasted_iota(jnp.int32, (GMLP_BLOCK, GMLP_BLOCK), 0)
                >= lax.broadcasted_iota(jnp.int32, (GMLP_BLOCK, GMLP_BLOCK), 1))
        group_lanes = width // GMLP_GROUPS
        for c in range(COL_BLOCK // LANE):
            g = lax.div(jj * COL_BLOCK + c * LANE, group_lanes)
            wt = jnp.where(tril, ws_ref[g], 0.0).astype(BF16)
            bias = bs_ref[g]
            for r in range(rows // GMLP_BLOCK):
                rs_, cs_ = slice(r * GMLP_BLOCK, (r + 1) * GMLP_BLOCK), slice(c * LANE, (c + 1) * LANE)
                fbuf_ref[rs_, cs_] = _dot(wt, vn[rs_, cs_]) + bias
        y_ref[...] = (u * fbuf_ref[...] * _silu(z)).astype(BF16)

    @pl.when(j >= 2 * n_branch)
    def _():
        h = xg_ref[...]
        rs = rstd_ref[...]
        q = _dot(h, wa_ref[...]) * rs
        z = _dot(h, wz_ref[...]) * rs
        y_ref[...] = _memory_read(q, z, kvh_ref, j - 2 * n_branch).astype(BF16)


def _gmlp_mixer(xg, rstd, w_in, ln_g, ln_b, w_s, b_s, kvh, rows=1024):
    s, d = xg.shape
    n_mix = d // COL_BLOCK
    n_branch = ln_g.shape[-1] // COL_BLOCK
    n_steps = n_branch + n_mix
    z0 = 2 * n_branch + (n_mix - n_branch)
    assert (n_branch * COL_BLOCK // GMLP_GROUPS) % LANE == 0 and rows % GMLP_BLOCK == 0
    lng = ln_g.reshape(n_branch, 1, COL_BLOCK)
    lnb = ln_b.reshape(n_branch, 1, COL_BLOCK)
    bs = b_s.reshape(GMLP_GROUPS, GMLP_BLOCK, 1)
    full = lambda a: pl.BlockSpec(a.shape, lambda i, j: (0,) * a.ndim)

    def wa_index(i, j):
        return 0, jnp.where(j < n_branch, j + n_branch, jnp.where(j < 2 * n_branch, j - n_branch, j))

    return pl.pallas_call(
        functools.partial(_gmlp_mixer_kernel, n_branch),
        grid=(s // rows, n_steps),
        in_specs=[
            pl.BlockSpec((rows, d), lambda i, j: (i, 0)),
            pl.BlockSpec((rows, 1), lambda i, j: (i, 0)),
            pl.BlockSpec((d, COL_BLOCK), wa_index),
            pl.BlockSpec((d, COL_BLOCK), lambda i, j: (0, z0 + jnp.maximum(j - n_branch, 0))),
            full(lng), full(lnb), full(w_s), full(bs), full(kvh),
        ],
        out_specs=pl.BlockSpec((rows, COL_BLOCK), lambda i, j: (i, jnp.maximum(j - n_branch, 0))),
        out_shape=jax.ShapeDtypeStruct((s, d), BF16),
        scratch_shapes=[
            pltpu.VMEM((n_branch, rows, COL_BLOCK), F32),
            pltpu.VMEM((rows, COL_BLOCK), F32),
            pltpu.VMEM((rows, 1), F32),
            pltpu.VMEM((rows, 1), F32),
        ],
        compiler_params=_compiler_params(2),
        name="gmlp_mixer",
    )(xg, rstd, w_in, w_in, lng, lnb, w_s, bs, kvh)


def _out_proj_kernel(n_tiles, emit_next, y_ref, w_ref, x_ref, gpost_ref, gnext_ref, *rest):
    if emit_next:
        xo_ref, xg_ref, rstd_ref, acc_ref, ss_ref, rs_prev_ref, ss2_ref = rest
    else:
        xo_ref, acc_ref, ss_ref, rs_prev_ref = rest
    i = pl.program_id(0)
    n = pl.program_id(1)
    n_blocks = pl.num_programs(1)
    d = acc_ref.shape[1]
    col = pl.ds(pl.multiple_of(n * COL_BLOCK, COL_BLOCK), COL_BLOCK)

    def finish_previous():
        xo = x_ref[...] + acc_ref[:, col] * rs_prev_ref[...] * gpost_ref[:, col]
        xo_ref[...] = xo
        if emit_next:
            xg_ref[...] = (xo * gnext_ref[:, col]).astype(BF16)
            part = jnp.sum(xo * xo, axis=-1, keepdims=True)
            ss2_ref[...] = jnp.where(n == 0, part, ss2_ref[...] + part)

    def project_current():
        r = _dot(y_ref[...], w_ref[...])
        acc_ref[:, col] = r
        part = jnp.sum(r * r, axis=-1, keepdims=True)
        ss_ref[...] = jnp.where(n == 0, part, ss_ref[...] + part)

    @pl.when(i == 0)
    def _():
        project_current()

    @pl.when(jnp.logical_and(i > 0, i < n_tiles))
    def _():
        finish_previous()
        project_current()

    @pl.when(i == n_tiles)
    def _():
        finish_previous()

    @pl.when(n == n_blocks - 1)
    def _():
        if emit_next:
            @pl.when(i > 0)
            def _():
                rstd_ref[...] = lax.rsqrt(ss2_ref[...] * (1.0 / d) + RMS_EPS)
        rs_prev_ref[...] = lax.rsqrt(ss_ref[...] * (1.0 / d) + RMS_EPS)


def _out_proj(y, w_out, x, g_post, g_next, rows=1024):
    s, d = x.shape
    n_tiles = s // rows
    n_blocks = d // COL_BLOCK
    emit_next = g_next is not None
    if g_next is None:
        g_next = g_post
    prev_tile = lambda i: jnp.maximum(i - 1, 0)
    prev_block = lambda i, n: (prev_tile(i), jnp.where(i > 0, n, 0))
    out_specs = [pl.BlockSpec((rows, COL_BLOCK), prev_block)]
    out_shape = [jax.ShapeDtypeStruct((s, d), F32)]
    scratch = [pltpu.VMEM((rows, d), F32), pltpu.VMEM((rows, 1), F32), pltpu.VMEM((rows, 1), F32)]
    if emit_next:
        out_specs += [pl.BlockSpec((rows, COL_BLOCK), prev_block),
                      pl.BlockSpec((rows, 1), lambda i, n: (prev_tile(i), 0))]
        out_shape += [jax.ShapeDtypeStruct((s, d), BF16), jax.ShapeDtypeStruct((s, 1), F32)]
        scratch += [pltpu.VMEM((rows, 1), F32)]
    return pl.pallas_call(
        functools.partial(_out_proj_kernel, n_tiles, emit_next),
        grid=(n_tiles + 1, n_blocks),
        in_specs=[
            pl.BlockSpec((rows, d), lambda i, n: (jnp.minimum(i, n_tiles - 1), 0)),
            pl.BlockSpec((d, COL_BLOCK), lambda i, n: (0, jnp.where(i < n_tiles, n, n_blocks - 1))),
            pl.BlockSpec((rows, COL_BLOCK), prev_block),
            pl.BlockSpec((1, d), lambda i, n: (0, 0)),
            pl.BlockSpec((1, d), lambda i, n: (0, 0)),
        ],
        out_specs=out_specs,
        out_shape=out_shape,
        scratch_shapes=scratch,
        compiler_params=_compiler_params(2),
        name="out_proj",
    )(y, w_out, x, g_post.reshape(1, d), g_next.reshape(1, d))


def kernel(x, mem, pre_norm_g, post_norm_g, mem_norm_g, w_mem_kv, w_out, conv_w_in, conv_w, gmlp_w_in,
           gmlp_ln_g, gmlp_ln_b, gmlp_w_s, gmlp_b_s):
    batch, s, d = x.shape
    assert batch == 1 and mem.shape[0] == 1
    depth = w_out.shape[0]
    xs = x.reshape(s, d)
    kvh = _memory_kv(mem.reshape(mem.shape[1], d), mem_norm_g, w_mem_kv)
    xg, rstd = _prenorm(xs, pre_norm_g[0])
    for i in range(depth):
        j = i // 2
        if i % 2 == 0:
            y = _conv_mixer(xg, rstd, conv_w_in[j].astype(BF16), conv_w[j], kvh[i])
        else:
            y = _gmlp_mixer(xg, rstd, gmlp_w_in[j].astype(BF16), gmlp_ln_g[j], gmlp_ln_b[j],
                            gmlp_w_s[j], gmlp_b_s[j], kvh[i])
        g_next = pre_norm_g[i + 1] if i + 1 < depth else None
        outs = _out_proj(y, w_out[i].astype(BF16), xs, post_norm_g[i], g_next)
        xs = outs[0]
        if g_next is not None:
            xg, rstd = outs[1], outs[2]
    return xs.reshape(batch, s, d)
```

```python
import functools

import jax
import jax.numpy as jnp
from jax import lax
from jax.experimental import pallas as pl
from jax.experimental.pallas import tpu as pltpu

F32 = jnp.float32
BF16 = jnp.bfloat16

RMS_EPS = 1e-6
LN_EPS = 1e-5

LANE = 128
SUBLANE = 8
COL_BLOCK = 256
OUT_COL_BLOCK = 512
CAST_ROWS = 16
VMEM_LIMIT_BYTES = 56 * 2**20

MEM_HEADS = 4
CONV_WIDTH = 3
GMLP_BLOCK = 128
GMLP_GROUPS = 8


def _dot(a, b):
    return jnp.dot(a, b, preferred_element_type=F32)


def _silu(z):
    return z * (1.0 / (1.0 + jnp.exp(-z)))


def _compiler_params(n_axes):
    return pltpu.CompilerParams(
        dimension_semantics=("arbitrary",) * n_axes,
        vmem_limit_bytes=VMEM_LIMIT_BYTES,
    )


def _prenorm_kernel(x_ref, g_ref, xg_ref, rstd_ref):
    xf = x_ref[...]
    rstd_ref[...] = lax.rsqrt(jnp.mean(xf * xf, axis=-1, keepdims=True) + RMS_EPS)
    xg_ref[...] = (xf * g_ref[...]).astype(BF16)


def _prenorm(x, g, rows=256):
    s, d = x.shape
    return pl.pallas_call(
        _prenorm_kernel,
        grid=(s // rows,),
        in_specs=[
            pl.BlockSpec((rows, d), lambda i: (i, 0)),
            pl.BlockSpec((1, d), lambda i: (0, 0)),
        ],
        out_specs=[
            pl.BlockSpec((rows, d), lambda i: (i, 0)),
            pl.BlockSpec((rows, 1), lambda i: (i, 0)),
        ],
        out_shape=[
            jax.ShapeDtypeStruct((s, d), BF16),
            jax.ShapeDtypeStruct((s, 1), F32),
        ],
        compiler_params=_compiler_params(1),
        name="prenorm",
    )(x, g.reshape(1, d))


def _kv_kernel(mem_ref, g_ref, w_ref, kv_ref):
    mf = mem_ref[...]
    y = mf * lax.rsqrt(jnp.mean(mf * mf, axis=-1, keepdims=True) + RMS_EPS)
    y = (y * g_ref[...]).astype(BF16)
    kv_ref[...] = _dot(y, w_ref[...].astype(BF16)).astype(BF16)


def _memory_kv(mem, mem_norm_g, w_mem_kv):
    m, d = mem.shape
    depth, _, kv_width = w_mem_kv.shape
    kv = pl.pallas_call(
        _kv_kernel,
        grid=(depth, kv_width // COL_BLOCK),
        in_specs=[
            pl.BlockSpec((m, d), lambda l, n: (0, 0)),
            pl.BlockSpec((None, 1, d), lambda l, n: (l, 0, 0)),
            pl.BlockSpec((None, d, COL_BLOCK), lambda l, n: (l, 0, n)),
        ],
        out_specs=pl.BlockSpec((None, m, COL_BLOCK), lambda l, n: (l, 0, n)),
        out_shape=jax.ShapeDtypeStruct((depth, m, kv_width), BF16),
        compiler_params=_compiler_params(2),
        name="memory_kv",
    )(mem, mem_norm_g.reshape(depth, 1, d), w_mem_kv)
    head_dim = kv_width // (2 * MEM_HEADS)
    return kv.reshape(depth, m, 2 * MEM_HEADS, head_dim).transpose(0, 2, 1, 3)


def _memory_read(q, z, kvh_ref, head):
    k = kvh_ref[head]
    v = kvh_ref[MEM_HEADS + head]
    scale = k.shape[-1] ** -0.5
    s = lax.dot_general(q.astype(BF16), k, (((1,), (1,)), ((), ())), preferred_element_type=F32) * scale
    e = jnp.exp(s - jnp.max(s, axis=-1, keepdims=True))
    o = _dot(e.astype(BF16), v) / jnp.sum(e, axis=-1, keepdims=True)
    return o * _silu(z)


def _cast_kernel(w_ref, o_ref):
    o_ref[...] = w_ref[...].astype(BF16)


def _cast_layer(w, layer, rows=128):
    _, d, c = w.shape
    return pl.pallas_call(
        _cast_kernel,
        grid=(d // rows,),
        in_specs=[pl.BlockSpec((None, rows, c), lambda r: (layer, r, 0))],
        out_specs=pl.BlockSpec((rows, c), lambda r: (r, 0)),
        out_shape=jax.ShapeDtypeStruct((d, c), BF16),
        compiler_params=_compiler_params(1),
        name="cast_weights",
    )(w)


def _side_cast_specs(casts, n_tiles, n_steps):
    in_specs, out_specs, out_shape = [], [], []
    for w, layer in casts:
        _, d, c = w.shape
        n_slabs = d // CAST_ROWS
        assert n_slabs <= n_tiles * n_steps

        def slab(i, j, n_slabs=n_slabs):
            return jnp.minimum(i * n_steps + j, n_slabs - 1)

        in_specs.append(pl.BlockSpec((None, CAST_ROWS, c),
                                     lambda i, j, layer=layer, slab=slab: (layer, slab(i, j), 0)))
        out_specs.append(pl.BlockSpec((CAST_ROWS, c), lambda i, j, slab=slab: (slab(i, j), 0)))
        out_shape.append(jax.ShapeDtypeStruct((d, c), BF16))
    return in_specs, out_specs, out_shape


def _side_cast(cast_in_refs, cast_out_refs):
    for src, dst in zip(cast_in_refs, cast_out_refs):
        dst[...] = src[...].astype(BF16)


def _conv_mixer_kernel(n_branch, n_cast, xg_ref, rstd_ref, wa_ref, wb_ref, wc_ref, wz_ref, cw_ref,
                       kvh_ref, *rest):
    cast_in, (y_ref, *cast_out), (ubuf_ref, carry_ref) = (
        rest[:n_cast], rest[n_cast:2 * n_cast + 1], rest[2 * n_cast + 1:])
    i = pl.program_id(0)
    j = pl.program_id(1)
    rows = xg_ref.shape[0]
    _side_cast(cast_in, cast_out)

    @pl.when(jnp.logical_and(i == 0, j < n_branch))
    def _():
        carry_ref[j] = jnp.zeros(carry_ref.shape[1:], F32)

    @pl.when(j < n_branch)
    def _():
        h = xg_ref[...]
        rs = rstd_ref[...]
        b_gate = _dot(h, wa_ref[...]) * rs
        c_gate = _dot(h, wb_ref[...]) * rs
        hid = _dot(h, wc_ref[...]) * rs
        z = _dot(h, wz_ref[...]) * rs
        u = c_gate * hid
        ubuf_ref[0:SUBLANE, :] = carry_ref[j]
        ubuf_ref[SUBLANE:SUBLANE + rows, :] = u
        carry_ref[j] = u[rows - SUBLANE:rows, :]
        u1 = ubuf_ref[SUBLANE - 1:SUBLANE - 1 + rows, :]
        u2 = ubuf_ref[SUBLANE - 2:SUBLANE - 2 + rows, :]
        w = cw_ref[j]
        conv = u2 * w[0:1, :] + u1 * w[1:2, :] + u * w[2:3, :]
        y_ref[...] = (b_gate * conv * _silu(z)).astype(BF16)

    @pl.when(j >= n_branch)
    def _():
        h = xg_ref[...]
        rs = rstd_ref[...]
        q = _dot(h, wa_ref[...]) * rs
        z = _dot(h, wz_ref[...]) * rs
        y_ref[...] = _memory_read(q, z, kvh_ref, j - n_branch).astype(BF16)


def _conv_mixer(xg, rstd, w_in, conv_w, kvh, casts, rows=1024):
    s, d = xg.shape
    n_mix = d // COL_BLOCK
    n_branch = conv_w.shape[-1] // COL_BLOCK
    last = n_branch - 1
    z0 = 3 * n_branch + (n_mix - n_branch)
    cw = conv_w.reshape(CONV_WIDTH, n_branch, COL_BLOCK).transpose(1, 0, 2)
    wspec = lambda fn: pl.BlockSpec((d, COL_BLOCK), fn)
    cast_in_specs, cast_out_specs, cast_out_shape = _side_cast_specs(casts, s // rows, n_mix)
    return pl.pallas_call(
        functools.partial(_conv_mixer_kernel, n_branch, len(casts)),
        grid=(s // rows, n_mix),
        in_specs=[
            pl.BlockSpec((rows, d), lambda i, j: (i, 0)),
            pl.BlockSpec((rows, 1), lambda i, j: (i, 0)),
            wspec(lambda i, j: (0, jnp.where(j < n_branch, j, j + 2 * n_branch))),
            wspec(lambda i, j: (0, n_branch + jnp.minimum(j, last))),
            wspec(lambda i, j: (0, 2 * n_branch + jnp.minimum(j, last))),
            wspec(lambda i, j: (0, z0 + j)),
            pl.BlockSpec(cw.shape, lambda i, j: (0, 0, 0)),
            pl.BlockSpec(kvh.shape, lambda i, j: (0, 0, 0)),
        ] + cast_in_specs,
        out_specs=[pl.BlockSpec((rows, COL_BLOCK), lambda i, j: (i, j))] + cast_out_specs,
        out_shape=[jax.ShapeDtypeStruct((s, d), BF16)] + cast_out_shape,
        scratch_shapes=[
            pltpu.VMEM((rows + SUBLANE, COL_BLOCK), F32),
            pltpu.VMEM((n_branch, SUBLANE, COL_BLOCK), F32),
        ],
        compiler_params=_compiler_params(2),
        name="conv_mixer",
    )(xg, rstd, w_in, w_in, w_in, w_in, cw, kvh, *[w for w, _ in casts])


def _gmlp_mixer_kernel(n_branch, n_cast, xg_ref, rstd_ref, wa_ref, wz_ref, lng_ref, lnb_ref, ws_ref,
                       bs_ref, kvh_ref, *rest):
    cast_in, (y_ref, *cast_out), (vbuf_ref, fbuf_ref, s1_ref, s2_ref) = (
        rest[:n_cast], rest[n_cast:2 * n_cast + 1], rest[2 * n_cast + 1:])
    j = pl.program_id(1)
    rows = xg_ref.shape[0]
    width = n_branch * COL_BLOCK
    _side_cast(cast_in, cast_out)

    @pl.when(j < n_branch)
    def _():
        v = jax.nn.gelu(_dot(xg_ref[...], wa_ref[...]) * rstd_ref[...])
        vbuf_ref[j] = v
        p1 = jnp.sum(v, axis=-1, keepdims=True)
        p2 = jnp.sum(v * v, axis=-1, keepdims=True)

        @pl.when(j == 0)
        def _():
            s1_ref[...] = p1
            s2_ref[...] = p2

        @pl.when(j > 0)
        def _():
            s1_ref[...] += p1
            s2_ref[...] += p2

    @pl.when(jnp.logical_and(j >= n_branch, j < 2 * n_branch))
    def _():
        jj = j - n_branch
        h = xg_ref[...]
        rs = rstd_ref[...]
        u = jax.nn.gelu(_dot(h, wa_ref[...]) * rs)
        z = _dot(h, wz_ref[...]) * rs
        mu = s1_ref[...] * (1.0 / width)
        var = s2_ref[...] * (1.0 / width) - mu * mu
        vn = (vbuf_ref[jj] - mu) * lax.rsqrt(var + LN_EPS) * lng_ref[jj] + lnb_ref[jj]
        vn = vn.astype(BF16)
        tril = (lax.broadcasted_iota(jnp.int32, (GMLP_BLOCK, GMLP_BLOCK), 0)
                >= lax.broadcasted_iota(jnp.int32, (GMLP_BLOCK, GMLP_BLOCK), 1))
        group_lanes = width // GMLP_GROUPS
        for c in range(COL_BLOCK // LANE):
            g = lax.div(jj * COL_BLOCK + c * LANE, group_lanes)
            wt = jnp.where(tril, ws_ref[g], 0.0).astype(BF16)
            bias = bs_ref[g]
            for r in range(rows // GMLP_BLOCK):
                rs_, cs_ = slice(r * GMLP_BLOCK, (r + 1) * GMLP_BLOCK), slice(c * LANE, (c + 1) * LANE)
                fbuf_ref[rs_, cs_] = _dot(wt, vn[rs_, cs_]) + bias
        y_ref[...] = (u * fbuf_ref[...] * _silu(z)).astype(BF16)

    @pl.when(j >= 2 * n_branch)
    def _():
        h = xg_ref[...]
        rs = rstd_ref[...]
        q = _dot(h, wa_ref[...]) * rs
        z = _dot(h, wz_ref[...]) * rs
        y_ref[...] = _memory_read(q, z, kvh_ref, j - 2 * n_branch).astype(BF16)


def _gmlp_mixer(xg, rstd, w_in, ln_g, ln_b, w_s, b_s, kvh, casts, rows=1024):
    s, d = xg.shape
    n_mix = d // COL_BLOCK
    n_branch = ln_g.shape[-1] // COL_BLOCK
    n_steps = n_branch + n_mix
    z0 = 2 * n_branch + (n_mix - n_branch)
    assert (n_branch * COL_BLOCK // GMLP_GROUPS) % LANE == 0 and rows % GMLP_BLOCK == 0
    lng = ln_g.reshape(n_branch, 1, COL_BLOCK)
    lnb = ln_b.reshape(n_branch, 1, COL_BLOCK)
    bs = b_s.reshape(GMLP_GROUPS, GMLP_BLOCK, 1)
    full = lambda a: pl.BlockSpec(a.shape, lambda i, j: (0,) * a.ndim)

    def wa_index(i, j):
        return 0, jnp.where(j < n_branch, j + n_branch, jnp.where(j < 2 * n_branch, j - n_branch, j))

    cast_in_specs, cast_out_specs, cast_out_shape = _side_cast_specs(casts, s // rows, n_steps)
    return pl.pallas_call(
        functools.partial(_gmlp_mixer_kernel, n_branch, len(casts)),
        grid=(s // rows, n_steps),
        in_specs=[
            pl.BlockSpec((rows, d), lambda i, j: (i, 0)),
            pl.BlockSpec((rows, 1), lambda i, j: (i, 0)),
            pl.BlockSpec((d, COL_BLOCK), wa_index),
            pl.BlockSpec((d, COL_BLOCK), lambda i, j: (0, z0 + jnp.maximum(j - n_branch, 0))),
            full(lng), full(lnb), full(w_s), full(bs), full(kvh),
        ] + cast_in_specs,
        out_specs=[pl.BlockSpec((rows, COL_BLOCK), lambda i, j: (i, jnp.maximum(j - n_branch, 0)))]
        + cast_out_specs,
        out_shape=[jax.ShapeDtypeStruct((s, d), BF16)] + cast_out_shape,
        scratch_shapes=[
            pltpu.VMEM((n_branch, rows, COL_BLOCK), F32),
            pltpu.VMEM((rows, COL_BLOCK), F32),
            pltpu.VMEM((rows, 1), F32),
            pltpu.VMEM((rows, 1), F32),
        ],
        compiler_params=_compiler_params(2),
        name="gmlp_mixer",
    )(xg, rstd, w_in, w_in, lng, lnb, w_s, bs, kvh, *[w for w, _ in casts])


def _out_proj_kernel(n_tiles, emit_next, y_ref, w_ref, x_ref, gpost_ref, gnext_ref, *rest):
    if emit_next:
        xo_ref, xg_ref, rstd_ref, acc_ref, ss_ref, rs_prev_ref, ss2_ref = rest
    else:
        xo_ref, acc_ref, ss_ref, rs_prev_ref = rest
    i = pl.program_id(0)
    n = pl.program_id(1)
    n_blocks = pl.num_programs(1)
    d = acc_ref.shape[1]
    width = w_ref.shape[1]
    col = pl.ds(pl.multiple_of(n * width, width), width)

    def finish_previous():
        xo = x_ref[...] + acc_ref[:, col] * rs_prev_ref[...] * gpost_ref[:, col]
        xo_ref[...] = xo
        if emit_next:
            xg_ref[...] = (xo * gnext_ref[:, col]).astype(BF16)
            part = jnp.sum(xo * xo, axis=-1, keepdims=True)
            ss2_ref[...] = jnp.where(n == 0, part, ss2_ref[...] + part)

    def project_current():
        r = _dot(y_ref[...], w_ref[...])
        acc_ref[:, col] = r
        part = jnp.sum(r * r, axis=-1, keepdims=True)
        ss_ref[...] = jnp.where(n == 0, part, ss_ref[...] + part)

    @pl.when(i == 0)
    def _():
        project_current()

    @pl.when(jnp.logical_and(i > 0, i < n_tiles))
    def _():
        finish_previous()
        project_current()

    @pl.when(i == n_tiles)
    def _():
        finish_previous()

    @pl.when(n == n_blocks - 1)
    def _():
        if emit_next:
            @pl.when(i > 0)
            def _():
                rstd_ref[...] = lax.rsqrt(ss2_ref[...] * (1.0 / d) + RMS_EPS)
        rs_prev_ref[...] = lax.rsqrt(ss_ref[...] * (1.0 / d) + RMS_EPS)


def _out_proj(y, w_out, x, g_post, g_next, rows=1024):
    s, d = x.shape
    n_tiles = s // rows
    width = OUT_COL_BLOCK
    n_blocks = d // width
    emit_next = g_next is not None
    if g_next is None:
        g_next = g_post
    prev_tile = lambda i: jnp.maximum(i - 1, 0)
    prev_block = lambda i, n: (prev_tile(i), jnp.where(i > 0, n, 0))
    out_specs = [pl.BlockSpec((rows, width), prev_block)]
    out_shape = [jax.ShapeDtypeStruct((s, d), F32)]
    scratch = [pltpu.VMEM((rows, d), F32), pltpu.VMEM((rows, 1), F32), pltpu.VMEM((rows, 1), F32)]
    if emit_next:
        out_specs += [pl.BlockSpec((rows, width), prev_block),
                      pl.BlockSpec((rows, 1), lambda i, n: (prev_tile(i), 0))]
        out_shape += [jax.ShapeDtypeStruct((s, d), BF16), jax.ShapeDtypeStruct((s, 1), F32)]
        scratch += [pltpu.VMEM((rows, 1), F32)]
    return pl.pallas_call(
        functools.partial(_out_proj_kernel, n_tiles, emit_next),
        grid=(n_tiles + 1, n_blocks),
        in_specs=[
            pl.BlockSpec((rows, d), lambda i, n: (jnp.minimum(i, n_tiles - 1), 0)),
            pl.BlockSpec((d, width), lambda i, n: (0, jnp.where(i < n_tiles, n, n_blocks - 1))),
            pl.BlockSpec((rows, width), prev_block),
            pl.BlockSpec((1, d), lambda i, n: (0, 0)),
            pl.BlockSpec((1, d), lambda i, n: (0, 0)),
        ],
        out_specs=out_specs,
        out_shape=out_shape,
        scratch_shapes=scratch,
        compiler_params=_compiler_params(2),
        name="out_proj",
    )(y, w_out, x, g_post.reshape(1, d), g_next.reshape(1, d))


def kernel(x, mem, pre_norm_g, post_norm_g, mem_norm_g, w_mem_kv, w_out, conv_w_in, conv_w, gmlp_w_in,
           gmlp_ln_g, gmlp_ln_b, gmlp_w_s, gmlp_b_s):
    batch, s, d = x.shape
    assert batch == 1 and mem.shape[0] == 1
    depth = w_out.shape[0]
    xs = x.reshape(s, d)
    kvh = _memory_kv(mem.reshape(mem.shape[1], d), mem_norm_g, w_mem_kv)
    xg, rstd = _prenorm(xs, pre_norm_g[0])
    w_in = _cast_layer(conv_w_in, 0)
    for i in range(depth):
        j = i // 2
        casts = [(w_out, i)]
        if i + 1 < depth:
            casts.append((gmlp_w_in, j) if i % 2 == 0 else (conv_w_in, j + 1))
        if i % 2 == 0:
            y, w_out_i, *w_next = _conv_mixer(xg, rstd, w_in, conv_w[j], kvh[i], casts)
        else:
            y, w_out_i, *w_next = _gmlp_mixer(xg, rstd, w_in, gmlp_ln_g[j], gmlp_ln_b[j],
                                              gmlp_w_s[j], gmlp_b_s[j], kvh[i], casts)
        if w_next:
            w_in = w_next[0]
        g_next = pre_norm_g[i + 1] if i + 1 < depth else None
        outs = _out_proj(y, w_out_i, xs, post_norm_g[i], g_next)
        xs = outs[0]
        if g_next is not None:
            xg, rstd = outs[1], outs[2]
    return xs.reshape(batch, s, d)
```

```python
import functools

import jax
import jax.numpy as jnp
from jax import lax
from jax.experimental import pallas as pl
from jax.experimental.pallas import tpu as pltpu

F32 = jnp.float32
BF16 = jnp.bfloat16

RMS_EPS = 1e-6
LN_EPS = 1e-5

LANE = 128
SUBLANE = 8
COL_BLOCK = 256
OUT_COL_BLOCK = 512
GMLP_COL_BLOCK = 512
BF16_SUBLANES = 16
VMEM_LIMIT_BYTES = 58 * 2**20

MEM_HEADS = 4
CONV_WIDTH = 3
GMLP_BLOCK = 128
GMLP_GROUPS = 8


def _dot(a, b):
    return jnp.dot(a, b, preferred_element_type=F32)


def _silu(z):
    return z * (1.0 / (1.0 + jnp.exp(-z)))


def _compiler_params(n_axes):
    return pltpu.CompilerParams(
        dimension_semantics=("arbitrary",) * n_axes,
        vmem_limit_bytes=VMEM_LIMIT_BYTES,
    )


def _prenorm_kernel(x_ref, g_ref, xg_ref, rstd_ref):
    xf = x_ref[...]
    rstd_ref[...] = lax.rsqrt(jnp.mean(xf * xf, axis=-1, keepdims=True) + RMS_EPS)
    xg_ref[...] = (xf * g_ref[...]).astype(BF16)


def _prenorm(x, g, rows=256):
    s, d = x.shape
    return pl.pallas_call(
        _prenorm_kernel,
        grid=(s // rows,),
        in_specs=[
            pl.BlockSpec((rows, d), lambda i: (i, 0)),
            pl.BlockSpec((1, d), lambda i: (0, 0)),
        ],
        out_specs=[
            pl.BlockSpec((rows, d), lambda i: (i, 0)),
            pl.BlockSpec((rows, 1), lambda i: (i, 0)),
        ],
        out_shape=[
            jax.ShapeDtypeStruct((s, d), BF16),
            jax.ShapeDtypeStruct((s, 1), F32),
        ],
        compiler_params=_compiler_params(1),
        name="prenorm",
    )(x, g.reshape(1, d))


def _kv_kernel(mem_ref, g_ref, w_ref, kv_ref):
    mf = mem_ref[...]
    y = mf * lax.rsqrt(jnp.mean(mf * mf, axis=-1, keepdims=True) + RMS_EPS)
    y = (y * g_ref[...]).astype(BF16)
    kv_ref[...] = _dot(y, w_ref[...].astype(BF16)).astype(BF16)


def _memory_kv(mem, mem_norm_g, w_mem_kv):
    m, d = mem.shape
    depth, _, kv_width = w_mem_kv.shape
    kv = pl.pallas_call(
        _kv_kernel,
        grid=(depth, kv_width // COL_BLOCK),
        in_specs=[
            pl.BlockSpec((m, d), lambda l, n: (0, 0)),
            pl.BlockSpec((None, 1, d), lambda l, n: (l, 0, 0)),
            pl.BlockSpec((None, d, COL_BLOCK), lambda l, n: (l, 0, n)),
        ],
        out_specs=pl.BlockSpec((None, m, COL_BLOCK), lambda l, n: (l, 0, n)),
        out_shape=jax.ShapeDtypeStruct((depth, m, kv_width), BF16),
        compiler_params=_compiler_params(2),
        name="memory_kv",
    )(mem, mem_norm_g.reshape(depth, 1, d), w_mem_kv)
    head_dim = kv_width // (2 * MEM_HEADS)
    return kv.reshape(depth, m, 2 * MEM_HEADS, head_dim).transpose(0, 2, 1, 3)


def _memory_read(q, z, kvh_ref, head):
    k = kvh_ref[head]
    v = kvh_ref[MEM_HEADS + head]
    scale = k.shape[-1] ** -0.5
    s = lax.dot_general(q.astype(BF16), k, (((1,), (1,)), ((), ())), preferred_element_type=F32) * scale
    e = jnp.exp(s - jnp.max(s, axis=-1, keepdims=True))
    o = _dot(e.astype(BF16), v) / jnp.sum(e, axis=-1, keepdims=True)
    return o * _silu(z)


def _cast_kernel(w_ref, o_ref):
    o_ref[...] = w_ref[...].astype(BF16)


def _cast_layer(w, layer, rows=128):
    _, d, c = w.shape
    return pl.pallas_call(
        _cast_kernel,
        grid=(d // rows,),
        in_specs=[pl.BlockSpec((None, rows, c), lambda r: (layer, r, 0))],
        out_specs=pl.BlockSpec((rows, c), lambda r: (r, 0)),
        out_shape=jax.ShapeDtypeStruct((d, c), BF16),
        compiler_params=_compiler_params(1),
        name="cast_weights",
    )(w)


def _side_cast_specs(casts, n_tiles, n_steps):
    in_specs, out_specs, out_shape = [], [], []
    for w, layer in casts:
        _, d, c = w.shape
        slab_rows = BF16_SUBLANES
        while d // slab_rows > n_tiles * n_steps:
            slab_rows *= 2
        assert d % slab_rows == 0
        n_slabs = d // slab_rows

        def slab(i, j, n_slabs=n_slabs):
            return jnp.minimum(i * n_steps + j, n_slabs - 1)

        in_specs.append(pl.BlockSpec((None, slab_rows, c),
                                     lambda i, j, layer=layer, slab=slab: (layer, slab(i, j), 0)))
        out_specs.append(pl.BlockSpec((slab_rows, c), lambda i, j, slab=slab: (slab(i, j), 0)))
        out_shape.append(jax.ShapeDtypeStruct((d, c), BF16))
    return in_specs, out_specs, out_shape


def _side_cast(cast_in_refs, cast_out_refs):
    for src, dst in zip(cast_in_refs, cast_out_refs):
        dst[...] = src[...].astype(BF16)


def _conv_mixer_kernel(n_branch, n_cast, xg_ref, rstd_ref, wa_ref, wb_ref, wc_ref, wz_ref, cw_ref,
                       kvh_ref, *rest):
    cast_in, (y_ref, *cast_out), (ubuf_ref, carry_ref) = (
        rest[:n_cast], rest[n_cast:2 * n_cast + 1], rest[2 * n_cast + 1:])
    i = pl.program_id(0)
    j = pl.program_id(1)
    rows = xg_ref.shape[0]
    _side_cast(cast_in, cast_out)

    @pl.when(jnp.logical_and(i == 0, j < n_branch))
    def _():
        carry_ref[j] = jnp.zeros(carry_ref.shape[1:], F32)

    @pl.when(j < n_branch)
    def _():
        h = xg_ref[...]
        rs = rstd_ref[...]
        c_gate = _dot(h, wb_ref[...]) * rs
        hid = _dot(h, wc_ref[...]) * rs
        z = _dot(h, wz_ref[...]) * rs
        b_proj = _dot(h, wa_ref[...])
        u = c_gate * hid
        ubuf_ref[0:SUBLANE, :] = carry_ref[j]
        ubuf_ref[SUBLANE:SUBLANE + rows, :] = u
        carry_ref[j] = u[rows - SUBLANE:rows, :]
        u1 = ubuf_ref[SUBLANE - 1:SUBLANE - 1 + rows, :]
        u2 = ubuf_ref[SUBLANE - 2:SUBLANE - 2 + rows, :]
        w = cw_ref[j]
        conv = u2 * w[0:1, :] + u1 * w[1:2, :] + u * w[2:3, :]
        y_ref[...] = (b_proj * (conv * _silu(z) * rs)).astype(BF16)

    @pl.when(j >= n_branch)
    def _():
        h = xg_ref[...]
        rs = rstd_ref[...]
        q = _dot(h, wa_ref[...]) * rs
        z = _dot(h, wz_ref[...]) * rs
        y_ref[...] = _memory_read(q, z, kvh_ref, j - n_branch).astype(BF16)


def _conv_mixer(xg, rstd, w_in, conv_w, kvh, casts, rows=1024):
    s, d = xg.shape
    n_mix = d // COL_BLOCK
    n_branch = conv_w.shape[-1] // COL_BLOCK
    last = n_branch - 1
    z0 = 3 * n_branch + (n_mix - n_branch)
    cw = conv_w.reshape(CONV_WIDTH, n_branch, COL_BLOCK).transpose(1, 0, 2)
    wspec = lambda fn: pl.BlockSpec((d, COL_BLOCK), fn)
    cast_in_specs, cast_out_specs, cast_out_shape = _side_cast_specs(casts, s // rows, n_mix)
    return pl.pallas_call(
        functools.partial(_conv_mixer_kernel, n_branch, len(casts)),
        grid=(s // rows, n_mix),
        in_specs=[
            pl.BlockSpec((rows, d), lambda i, j: (i, 0)),
            pl.BlockSpec((rows, 1), lambda i, j: (i, 0)),
            wspec(lambda i, j: (0, jnp.where(j < n_branch, j, j + 2 * n_branch))),
            wspec(lambda i, j: (0, n_branch + jnp.minimum(j, last))),
            wspec(lambda i, j: (0, 2 * n_branch + jnp.minimum(j, last))),
            wspec(lambda i, j: (0, z0 + j)),
            pl.BlockSpec(cw.shape, lambda i, j: (0, 0, 0)),
            pl.BlockSpec(kvh.shape, lambda i, j: (0, 0, 0)),
        ] + cast_in_specs,
        out_specs=[pl.BlockSpec((rows, COL_BLOCK), lambda i, j: (i, j))] + cast_out_specs,
        out_shape=[jax.ShapeDtypeStruct((s, d), BF16)] + cast_out_shape,
        scratch_shapes=[
            pltpu.VMEM((rows + SUBLANE, COL_BLOCK), F32),
            pltpu.VMEM((n_branch, SUBLANE, COL_BLOCK), F32),
        ],
        compiler_params=_compiler_params(2),
        name="conv_mixer",
    )(xg, rstd, w_in, w_in, w_in, w_in, cw, kvh, *[w for w, _ in casts])


def _gmlp_mixer_kernel(n_stage, n_emit, n_cast, xg_ref, rstd_ref, wa_ref, wz_ref, lng_ref, lnb_ref,
                       ws_ref, bs_ref, kvh_ref, *rest):
    cast_in, (y_ref, *cast_out), (vbuf_ref, s1_ref, s2_ref) = (
        rest[:n_cast], rest[n_cast:2 * n_cast + 1], rest[2 * n_cast + 1:])
    j = pl.program_id(1)
    rows = xg_ref.shape[0]
    cb = wa_ref.shape[1]
    width = n_emit * cb
    _side_cast(cast_in, cast_out)

    @pl.when(j < n_stage)
    def _():
        h = xg_ref[...]
        rs = rstd_ref[...]
        va = jax.nn.gelu(_dot(h, wa_ref[...]) * rs)
        vb = jax.nn.gelu(_dot(h, wz_ref[...]) * rs)
        vbuf_ref[2 * j] = va.astype(BF16)
        vbuf_ref[2 * j + 1] = vb.astype(BF16)
        p1 = jnp.sum(va, axis=-1, keepdims=True) + jnp.sum(vb, axis=-1, keepdims=True)
        p2 = jnp.sum(va * va, axis=-1, keepdims=True) + jnp.sum(vb * vb, axis=-1, keepdims=True)

        @pl.when(j == 0)
        def _():
            s1_ref[...] = p1
            s2_ref[...] = p2

        @pl.when(j > 0)
        def _():
            s1_ref[...] += p1
            s2_ref[...] += p2

    @pl.when(jnp.logical_and(j >= n_stage, j < n_stage + n_emit))
    def _():
        jj = j - n_stage
        h = xg_ref[...]
        rs = rstd_ref[...]
        u = jax.nn.gelu(_dot(h, wa_ref[...]) * rs)
        z = _dot(h, wz_ref[...]) * rs
        mu = s1_ref[...] * (1.0 / width)
        var = s2_ref[...] * (1.0 / width) - mu * mu
        vn = (vbuf_ref[jj].astype(F32) - mu) * lax.rsqrt(var + LN_EPS) * lng_ref[jj] + lnb_ref[jj]
        vn = vn.astype(BF16)
        tril = (lax.broadcasted_iota(jnp.int32, (GMLP_BLOCK, GMLP_BLOCK), 0)
                >= lax.broadcasted_iota(jnp.int32, (GMLP_BLOCK, GMLP_BLOCK), 1))
        group_lanes = width // GMLP_GROUPS
        gate = u * _silu(z)
        for c in range(cb // LANE):
            g = lax.div(jj * cb + c * LANE, group_lanes)
            wt = jnp.where(tril, ws_ref[g], 0.0).astype(BF16)
            bias = bs_ref[g]
            for r in range(rows // GMLP_BLOCK):
                rs_, cs_ = slice(r * GMLP_BLOCK, (r + 1) * GMLP_BLOCK), slice(c * LANE, (c + 1) * LANE)
                y_ref[rs_, cs_] = (gate[rs_, cs_] * (_dot(wt, vn[rs_, cs_]) + bias)).astype(BF16)

    @pl.when(j >= n_stage + n_emit)
    def _():
        h = xg_ref[...]
        rs = rstd_ref[...]
        q = _dot(h, wa_ref[...]) * rs
        z = _dot(h, wz_ref[...]) * rs
        head_dim = kvh_ref.shape[-1]
        heads_per_block = cb // head_dim
        for hh in range(heads_per_block):
            cs_ = slice(hh * head_dim, (hh + 1) * head_dim)
            head = (j - n_stage - n_emit) * heads_per_block + hh
            y_ref[:, cs_] = _memory_read(q[:, cs_], z[:, cs_], kvh_ref, head).astype(BF16)


def _gmlp_mixer(xg, rstd, w_in, ln_g, ln_b, w_s, b_s, kvh, casts, rows=1024):
    s, d = xg.shape
    cb = GMLP_COL_BLOCK
    n_emit = ln_g.shape[-1] // cb
    n_mem = d // cb - n_emit
    n_stage = n_emit // 2
    n_steps = n_stage + n_emit + n_mem
    z0 = 2 * n_emit + n_mem
    assert n_emit % 2 == 0 and (n_emit * cb // GMLP_GROUPS) % LANE == 0 and rows % GMLP_BLOCK == 0
    lng = ln_g.reshape(n_emit, 1, cb)
    lnb = ln_b.reshape(n_emit, 1, cb)
    bs = b_s.reshape(GMLP_GROUPS, GMLP_BLOCK, 1)
    full = lambda a: pl.BlockSpec(a.shape, lambda i, j: (0,) * a.ndim)

    def wa_index(i, j):
        return 0, jnp.where(j < n_stage, n_emit + 2 * j,
                            jnp.where(j < n_stage + n_emit, j - n_stage, j + n_emit - n_stage))

    def wz_index(i, j):
        return 0, jnp.where(j < n_stage, n_emit + 2 * j + 1, z0 + j - n_stage)

    cast_in_specs, cast_out_specs, cast_out_shape = _side_cast_specs(casts, s // rows, n_steps)
    return pl.pallas_call(
        functools.partial(_gmlp_mixer_kernel, n_stage, n_emit, len(casts)),
        grid=(s // rows, n_steps),
        in_specs=[
            pl.BlockSpec((rows, d), lambda i, j: (i, 0)),
            pl.BlockSpec((rows, 1), lambda i, j: (i, 0)),
            pl.BlockSpec((d, cb), wa_index),
            pl.BlockSpec((d, cb), wz_index),
            full(lng), full(lnb), full(w_s), full(bs), full(kvh),
        ] + cast_in_specs,
        out_specs=[pl.BlockSpec((rows, cb), lambda i, j: (i, jnp.maximum(j - n_stage, 0)))]
        + cast_out_specs,
        out_shape=[jax.ShapeDtypeStruct((s, d), BF16)] + cast_out_shape,
        scratch_shapes=[
            pltpu.VMEM((n_emit, rows, cb), BF16),
            pltpu.VMEM((rows, 1), F32),
            pltpu.VMEM((rows, 1), F32),
        ],
        compiler_params=_compiler_params(2),
        name="gmlp_mixer",
    )(xg, rstd, w_in, w_in, lng, lnb, w_s, bs, kvh, *[w for w, _ in casts])


def _out_proj_kernel(n_tiles, emit_next, y_ref, w_ref, x_ref, gpost_ref, gnext_ref, *rest):
    if emit_next:
        xo_ref, xg_ref, rstd_ref, acc_ref, ss_ref, rs_prev_ref, ss2_ref = rest
    else:
        xo_ref, acc_ref, ss_ref, rs_prev_ref = rest
    i = pl.program_id(0)
    n = pl.program_id(1)
    n_blocks = pl.num_programs(1)
    d = acc_ref.shape[1]
    width = w_ref.shape[1]
    col = pl.ds(pl.multiple_of(n * width, width), width)

    def finish_previous():
        xo = x_ref[...] + acc_ref[:, col] * rs_prev_ref[...] * gpost_ref[:, col]
        xo_ref[...] = xo
        if emit_next:
            xg_ref[...] = (xo * gnext_ref[:, col]).astype(BF16)
            part = jnp.sum(xo * xo, axis=-1, keepdims=True)
            ss2_ref[...] = jnp.where(n == 0, part, ss2_ref[...] + part)

    def project_current():
        r = _dot(y_ref[...], w_ref[...])
        acc_ref[:, col] = r
        part = jnp.sum(r * r, axis=-1, keepdims=True)
        ss_ref[...] = jnp.where(n == 0, part, ss_ref[...] + part)

    @pl.when(i == 0)
    def _():
        project_current()

    @pl.when(jnp.logical_and(i > 0, i < n_tiles))
    def _():
        finish_previous()
        project_current()

    @pl.when(i == n_tiles)
    def _():
        finish_previous()

    @pl.when(n == n_blocks - 1)
    def _():
        if emit_next:
            @pl.when(i > 0)
            def _():
                rstd_ref[...] = lax.rsqrt(ss2_ref[...] * (1.0 / d) + RMS_EPS)
        rs_prev_ref[...] = lax.rsqrt(ss_ref[...] * (1.0 / d) + RMS_EPS)


def _out_proj(y, w_out, x, g_post, g_next, rows=1024):
    s, d = x.shape
    n_tiles = s // rows
    width = OUT_COL_BLOCK
    n_blocks = d // width
    emit_next = g_next is not None
    if g_next is None:
        g_next = g_post
    prev_tile = lambda i: jnp.maximum(i - 1, 0)
    prev_block = lambda i, n: (prev_tile(i), jnp.where(i > 0, n, 0))
    out_specs = [pl.BlockSpec((rows, width), prev_block)]
    out_shape = [jax.ShapeDtypeStruct((s, d), F32)]
    scratch = [pltpu.VMEM((rows, d), F32), pltpu.VMEM((rows, 1), F32), pltpu.VMEM((rows, 1), F32)]
    if emit_next:
        out_specs += [pl.BlockSpec((rows, width), prev_block),
                      pl.BlockSpec((rows, 1), lambda i, n: (prev_tile(i), 0))]
        out_shape += [jax.ShapeDtypeStruct((s, d), BF16), jax.ShapeDtypeStruct((s, 1), F32)]
        scratch += [pltpu.VMEM((rows, 1), F32)]
    return pl.pallas_call(
        functools.partial(_out_proj_kernel, n_tiles, emit_next),
        grid=(n_tiles + 1, n_blocks),
        in_specs=[
            pl.BlockSpec((rows, d), lambda i, n: (jnp.minimum(i, n_tiles - 1), 0)),
            pl.BlockSpec((d, width), lambda i, n: (0, jnp.where(i < n_tiles, n, n_blocks - 1))),
            pl.BlockSpec((rows, width), prev_block),
            pl.BlockSpec((1, d), lambda i, n: (0, 0)),
            pl.BlockSpec((1, d), lambda i, n: (0, 0)),
        ],
        out_specs=out_specs,
        out_shape=out_shape,
        scratch_shapes=scratch,
        compiler_params=_compiler_params(2),
        name="out_proj",
    )(y, w_out, x, g_post.reshape(1, d), g_next.reshape(1, d))


def kernel(x, mem, pre_norm_g, post_norm_g, mem_norm_g, w_mem_kv, w_out, conv_w_in, conv_w, gmlp_w_in,
           gmlp_ln_g, gmlp_ln_b, gmlp_w_s, gmlp_b_s):
    batch, s, d = x.shape
    assert batch == 1 and mem.shape[0] == 1
    depth = w_out.shape[0]
    xs = x.reshape(s, d)
    kvh = _memory_kv(mem.reshape(mem.shape[1], d), mem_norm_g, w_mem_kv)
    xg, rstd = _prenorm(xs, pre_norm_g[0])
    w_in = {0: _cast_layer(conv_w_in, 0)}
    for i in range(depth):
        j = i // 2
        casts, cast_for = [(w_out, i)], []
        if i % 2 == 0:
            if i + 1 < depth:
                casts.append((gmlp_w_in, j))
                cast_for.append(i + 1)
            if i + 2 < depth:
                casts.append((conv_w_in, j + 1))
                cast_for.append(i + 2)
            y, w_out_i, *w_later = _conv_mixer(xg, rstd, w_in[i], conv_w[j], kvh[i], casts)
        else:
            y, w_out_i, *w_later = _gmlp_mixer(xg, rstd, w_in[i], gmlp_ln_g[j], gmlp_ln_b[j],
                                               gmlp_w_s[j], gmlp_b_s[j], kvh[i], casts)
        w_in.update(zip(cast_for, w_later))
        g_next = pre_norm_g[i + 1] if i + 1 < depth else None
        outs = _out_proj(y, w_out_i, xs, post_norm_g[i], g_next)
        xs = outs[0]
        if g_next is not None:
            xg, rstd = outs[1], outs[2]
    return xs.reshape(batch, s, d)
```
